```python
import math
import jax, jax.numpy as jnp
from jax import lax
import numpy as np

D_MODEL = 1024
BATCH = 8
SEQ = 4096
DEPTH = 2

N_MIXERS = 2
CHUNK = 128
D_FF = 2816
A_WIDTH = 2 * D_MODEL
A_GROUPS = 8
A_GROUP_DIM = A_WIDTH // A_GROUPS
B_HEADS = 4
B_QK_DIM = D_MODEL // B_HEADS
B_V_DIM = 2 * B_QK_DIM
B_V_WIDTH = B_HEADS * B_V_DIM
ROPE_BASE = 10000.0
ALPHA = float((2 * DEPTH) ** 0.25)
BETA = float((8 * DEPTH) ** -0.25)
N_A = (DEPTH + 1) // 2
N_B = DEPTH // 2
LN_EPS = 1e-5
GN_EPS = 1e-6

kernel_name = "deepnorm_macaron_gmlp_retention_hybrid"


def layer_norm(x, g, b):
    xf = x.astype(jnp.float32)
    mu = jnp.mean(xf, axis=-1, keepdims=True)
    var = jnp.mean(jnp.square(xf - mu), axis=-1, keepdims=True)
    y = (xf - mu) * lax.rsqrt(var + LN_EPS)
    return (y * g.astype(jnp.float32) + b.astype(jnp.float32)).astype(x.dtype)


def swiglu(x, w_gate, w_up, w_down):
    return (jax.nn.silu(x @ w_gate) * (x @ w_up)) @ w_down


def chunked_spatial_gating(x, w_in, ln_g, ln_b, w_s, b_s, w_out):
    B, S, _ = x.shape
    nc = S // CHUNK
    z = jax.nn.gelu(x @ w_in, approximate=False)
    u, v = jnp.split(z, 2, axis=-1)
    v = layer_norm(v, ln_g, ln_b)
    v = v.reshape(B, nc, CHUNK, A_GROUPS, A_GROUP_DIM)
    causal = jnp.tril(jnp.ones((CHUNK, CHUNK), dtype=bool))
    w = jnp.where(causal[None], w_s, jnp.zeros((), w_s.dtype)).astype(v.dtype)
    mixed = jnp.einsum('gts,bcsgd->bctgd', w, v)
    mixed = mixed + b_s.T.astype(v.dtype)[None, None, :, :, None]
    gated = u * mixed.reshape(B, S, A_WIDTH)
    return gated @ w_out


def rotary(x, pos):
    half = x.shape[-1] // 2
    inv = ROPE_BASE ** (-jnp.arange(half, dtype=jnp.float32) / half)
    ang = pos.astype(jnp.float32)[:, None] * inv[None, :]
    cos = jnp.cos(ang)[None, :, None, :].astype(x.dtype)
    sin = jnp.sin(ang)[None, :, None, :].astype(x.dtype)
    x1, x2 = x[..., :half], x[..., half:]
    return jnp.concatenate([x1 * cos - x2 * sin, x1 * sin + x2 * cos], axis=-1)


def retention(x, w_in, w_out):
    B, S, _ = x.shape
    nc = S // CHUNK
    proj = x @ w_in
    q, k, v, g = jnp.split(proj, [D_MODEL, 2 * D_MODEL, 2 * D_MODEL + B_V_WIDTH], axis=-1)
    pos = jnp.arange(S)
    q = rotary(q.reshape(B, S, B_HEADS, B_QK_DIM), pos).astype(jnp.float32)
    k = (rotary(k.reshape(B, S, B_HEADS, B_QK_DIM), pos).astype(jnp.float32)
         * (B_QK_DIM ** -0.5))
    v = v.reshape(B, S, B_HEADS, B_V_DIM).astype(jnp.float32)

    log_gamma = jnp.log1p(-jnp.exp2(-5.0 - jnp.arange(B_HEADS, dtype=jnp.float32)))
    idx = jnp.arange(CHUNK, dtype=jnp.float32)
    rel = idx[:, None] - idx[None, :]
    intra_decay = jnp.where(rel >= 0,
                            jnp.exp(log_gamma[:, None, None] * jnp.maximum(rel, 0.0)),
                            0.0)
    query_decay = jnp.exp(log_gamma[:, None] * (idx + 1.0))[None, :, :, None]
    key_decay = jnp.exp(log_gamma[:, None] * (CHUNK - 1.0 - idx))[None, :, :, None]
    chunk_decay = jnp.exp(log_gamma * CHUNK)[None, :, None, None]

    def to_chunks(t):
        return t.reshape(B, nc, CHUNK, B_HEADS, t.shape[-1]).transpose(1, 0, 3, 2, 4)

    def step(state, inp):
        qi, ki, vi = inp
        scores = jnp.einsum('bhtd,bhsd->bhts', qi, ki) * intra_decay[None]
        inner = jnp.einsum('bhts,bhsv->bhtv', scores, vi)
        cross = jnp.einsum('bhtd,bhdv->bhtv', qi, state) * query_decay
        new_state = state * chunk_decay + jnp.einsum('bhsd,bhsv->bhdv', ki * key_decay, vi)
        return new_state, inner + cross

    state0 = jnp.zeros((B, B_HEADS, B_QK_DIM, B_V_DIM), jnp.float32)
    _, y = lax.scan(step, state0, (to_chunks(q), to_chunks(k), to_chunks(v)))
    y = y.transpose(1, 0, 3, 2, 4).reshape(B, S, B_HEADS, B_V_DIM)
    mu = jnp.mean(y, axis=-1, keepdims=True)
    var = jnp.mean(jnp.square(y - mu), axis=-1, keepdims=True)
    y = ((y - mu) * lax.rsqrt(var + GN_EPS)).reshape(B, S, B_V_WIDTH).astype(x.dtype)
    return (jax.nn.silu(g) * y) @ w_out


def setup_inputs(seed: int = 0) -> dict:
    key = jax.random.key(seed)
    ks = jax.random.split(key, 16)
    f32 = jnp.float32
    nrm = lambda k, shape, scale: jax.random.normal(k, shape, f32) * scale
    x = jax.random.normal(ks[0], (BATCH, SEQ, D_MODEL), f32)
    ln_g = 1.0 + nrm(ks[1], (DEPTH, 3, D_MODEL), 0.1)
    ln_b = nrm(ks[2], (DEPTH, 3, D_MODEL), 0.02)
    ffn_w_gate = nrm(ks[3], (DEPTH, 2, D_MODEL, D_FF), D_MODEL ** -0.5)
    ffn_w_up = nrm(ks[4], (DEPTH, 2, D_MODEL, D_FF), D_MODEL ** -0.5)
    ffn_w_down = nrm(ks[5], (DEPTH, 2, D_FF, D_MODEL), BETA * D_FF ** -0.5)
    a_w_in = nrm(ks[6], (N_A, D_MODEL, 2 * A_WIDTH), D_MODEL ** -0.5)
    a_ln_g = 1.0 + nrm(ks[7], (N_A, A_WIDTH), 0.1)
    a_ln_b = nrm(ks[8], (N_A, A_WIDTH), 0.02)
    a_w_s = nrm(ks[9], (N_A, A_GROUPS, CHUNK, CHUNK), CHUNK ** -0.5)
    a_b_s = 1.0 + nrm(ks[10], (N_A, A_GROUPS, CHUNK), 0.1)
    a_w_out = nrm(ks[11], (N_A, A_WIDTH, D_MODEL), BETA * A_WIDTH ** -0.5)
    b_w_in = nrm(ks[12], (N_B, D_MODEL, 2 * D_MODEL + 2 * B_V_WIDTH), D_MODEL ** -0.5)
    b_w_out = nrm(ks[13], (N_B, B_V_WIDTH, D_MODEL), BETA * B_V_WIDTH ** -0.5)
    return {"x": x, "ln_g": ln_g, "ln_b": ln_b,
            "ffn_w_gate": ffn_w_gate, "ffn_w_up": ffn_w_up, "ffn_w_down": ffn_w_down,
            "a_w_in": a_w_in, "a_ln_g": a_ln_g, "a_ln_b": a_ln_b,
            "a_w_s": a_w_s, "a_b_s": a_b_s, "a_w_out": a_w_out,
            "b_w_in": b_w_in, "b_w_out": b_w_out}


def reference(x, ln_g, ln_b, ffn_w_gate, ffn_w_up, ffn_w_down,
              a_w_in, a_ln_g, a_ln_b, a_w_s, a_b_s, a_w_out,
              b_w_in, b_w_out):
    for i in range(DEPTH):
        f = swiglu(x, ffn_w_gate[i, 0], ffn_w_up[i, 0], ffn_w_down[i, 0])
        x = layer_norm(ALPHA * x + 0.5 * f, ln_g[i, 0], ln_b[i, 0])
        j = i // N_MIXERS
        if i % N_MIXERS == 0:
            h = chunked_spatial_gating(x, a_w_in[j], a_ln_g[j], a_ln_b[j],
                                       a_w_s[j], a_b_s[j], a_w_out[j])
        else:
            h = retention(x, b_w_in[j], b_w_out[j])
        x = layer_norm(ALPHA * x + h, ln_g[i, 1], ln_b[i, 1])
        f = swiglu(x, ffn_w_gate[i, 1], ffn_w_up[i, 1], ffn_w_down[i, 1])
        x = layer_norm(ALPHA * x + 0.5 * f, ln_g[i, 2], ln_b[i, 2])
    return x
```

```python
import functools
import math

import jax
import jax.numpy as jnp
import numpy as np
from jax import lax
from jax.experimental import pallas as pl
from jax.experimental.pallas import tpu as pltpu

D_MODEL = 1024
DEPTH = 2
CHUNK = 128
D_FF = 2816
A_WIDTH = 2 * D_MODEL
A_GROUPS = 8
A_GROUP_DIM = A_WIDTH // A_GROUPS
B_HEADS = 4
B_QK_DIM = D_MODEL // B_HEADS
B_V_DIM = 2 * B_QK_DIM
B_V_WIDTH = B_HEADS * B_V_DIM
ROPE_BASE = 10000.0
ALPHA = float((2 * DEPTH) ** 0.25)
LN_EPS = 1e-5
GN_EPS = 1e-6

BF16 = jnp.bfloat16
F32 = jnp.float32

VMEM_LIMIT_BYTES = 56 * 1024 * 1024
TOKEN_BLOCK = 512


def _layer_norm(y, g, b, eps):
    mu = jnp.mean(y, axis=-1, keepdims=True)
    yc = y - mu
    var = jnp.mean(yc * yc, axis=-1, keepdims=True)
    return yc * lax.rsqrt(var + eps) * g + b


def _resident(shape):
    zeros = (0,) * len(shape)
    return pl.BlockSpec(shape, lambda *_: zeros, pipeline_mode=pl.Buffered(1))


def _params():
    return pltpu.CompilerParams(
        dimension_semantics=("arbitrary",), vmem_limit_bytes=VMEM_LIMIT_BYTES)


def _ffn_kernel(x_ref, wg_ref, wu_ref, wd_ref, g_ref, b_ref, o_ref):
    x = x_ref[...]
    xb = x.astype(BF16)
    gate = jnp.dot(xb, wg_ref[...], preferred_element_type=F32)
    up = jnp.dot(xb, wu_ref[...], preferred_element_type=F32)
    h = (gate * jax.nn.sigmoid(gate) * up).astype(BF16)
    f = jnp.dot(h, wd_ref[...], preferred_element_type=F32)
    o_ref[...] = _layer_norm(ALPHA * x + 0.5 * f, g_ref[...], b_ref[...], LN_EPS)


def _ffn_call(x, wg, wu, wd, g, b):
    n = x.shape[0]
    row = pl.BlockSpec((TOKEN_BLOCK, D_MODEL), lambda i: (i, 0))
    return pl.pallas_call(
        _ffn_kernel,
        grid=(n // TOKEN_BLOCK,),
        in_specs=[row, _resident((D_MODEL, D_FF)), _resident((D_MODEL, D_FF)),
                  _resident((D_FF, D_MODEL)), _resident((1, D_MODEL)), _resident((1, D_MODEL))],
        out_specs=row,
        out_shape=jax.ShapeDtypeStruct((n, D_MODEL), F32),
        compiler_params=_params(),
        name="ffn_ln",
    )(x, wg, wu, wd, g, b)


def _sgu_kernel(x_ref, win_ref, vg_ref, vb_ref, ws_ref, bs_ref, wout_ref, g_ref, b_ref, o_ref,
                gated_ref):
    x = x_ref[...]
    z = jnp.dot(x.astype(BF16), win_ref[...], preferred_element_type=F32)
    z = 0.5 * z * (1.0 + lax.erf(z * np.float32(math.sqrt(0.5))))
    u = z[:, :A_WIDTH]
    v = _layer_norm(z[:, A_WIDTH:], vg_ref[...], vb_ref[...], LN_EPS).astype(BF16)

    t_idx = lax.broadcasted_iota(jnp.int32, (CHUNK, CHUNK), 0)
    s_idx = lax.broadcasted_iota(jnp.int32, (CHUNK, CHUNK), 1)
    causal = t_idx >= s_idx
    bias_tg = bs_ref[...]
    for grp in range(A_GROUPS):
        w = jnp.where(causal, ws_ref[grp], 0.0).astype(BF16)
        bias = jnp.broadcast_to(bias_tg[:, grp:grp + 1], (CHUNK, A_GROUP_DIM))
        cols = slice(grp * A_GROUP_DIM, (grp + 1) * A_GROUP_DIM)
        for c in range(TOKEN_BLOCK // CHUNK):
            rows = slice(c * CHUNK, (c + 1) * CHUNK)
            mixed = jnp.dot(w, v[rows, cols], preferred_element_type=F32) + bias
            gated_ref[rows, cols] = (u[rows, cols] * mixed).astype(BF16)

    h = jnp.dot(gated_ref[...], wout_ref[...], preferred_element_type=F32)
    o_ref[...] = _layer_norm(ALPHA * x + h, g_ref[...], b_ref[...], LN_EPS)


def _sgu_call(x, win, vg, vb, ws, bs_tg, wout, g, b):
    n = x.shape[0]
    row = pl.BlockSpec((TOKEN_BLOCK, D_MODEL), lambda i: (i, 0))
    return pl.pallas_call(
        _sgu_kernel,
        grid=(n // TOKEN_BLOCK,),
        in_specs=[row, _resident((D_MODEL, 2 * A_WIDTH)), _resident((1, A_WIDTH)),
                  _resident((1, A_WIDTH)), _resident((A_GROUPS, CHUNK, CHUNK)),
                  _resident((CHUNK, A_GROUPS)), _resident((A_WIDTH, D_MODEL)),
                  _resident((1, D_MODEL)), _resident((1, D_MODEL))],
        out_specs=row,
        out_shape=jax.ShapeDtypeStruct((n, D_MODEL), F32),
        scratch_shapes=[pltpu.VMEM((TOKEN_BLOCK, A_WIDTH), BF16)],
        compiler_params=_params(),
        name="sgu_ln",
    )(x, win, vg, vb, ws, bs_tg, wout, g, b)


def _rotary(t, cos, sin):
    half = B_QK_DIM // 2
    t1, t2 = t[:, :half], t[:, half:]
    return jnp.concatenate([t1 * cos - t2 * sin, t1 * sin + t2 * cos], axis=-1)


def _ret_kernel(cdec_ref, x_ref, win_ref, cos_ref, sin_ref, intra_ref, qdec_ref, kdec_ref,
                wout_ref, g_ref, b_ref, o_ref, state_ref, y_ref):
    @pl.when(pl.program_id(1) == 0)
    def _():
        state_ref[...] = jnp.zeros_like(state_ref)

    x = x_ref[...]
    proj = jnp.dot(x.astype(BF16), win_ref[...], preferred_element_type=F32)
    cos = cos_ref[...]
    sin = sin_ref[...]
    k_scale = np.float32(B_QK_DIM ** -0.5)

    for h in range(B_HEADS):
        qcols = slice(h * B_QK_DIM, (h + 1) * B_QK_DIM)
        kcols = slice(D_MODEL + h * B_QK_DIM, D_MODEL + (h + 1) * B_QK_DIM)
        vcols = slice(2 * D_MODEL + h * B_V_DIM, 2 * D_MODEL + (h + 1) * B_V_DIM)
        q = _rotary(proj[:, qcols], cos, sin)
        k = _rotary(proj[:, kcols], cos, sin) * k_scale
        v = proj[:, vcols].astype(BF16)
        intra = intra_ref[h]
        qdec = jnp.concatenate([qdec_ref[h]] * (B_V_DIM // CHUNK), axis=-1)
        kdec = jnp.concatenate([kdec_ref[h]] * (B_QK_DIM // CHUNK), axis=-1)
        cdec = cdec_ref[h]
        for c in range(TOKEN_BLOCK // CHUNK):
            rows = slice(c * CHUNK, (c + 1) * CHUNK)
            qi = q[rows].astype(BF16)
            ki = k[rows]
            vi = v[rows]
            state = state_ref[h]
            scores = lax.dot_general(qi, ki.astype(BF16), (((1,), (1,)), ((), ())),
                                     preferred_element_type=F32) * intra
            inner = jnp.dot(scores.astype(BF16), vi, preferred_element_type=F32)
            cross = jnp.dot(qi, state.astype(BF16), preferred_element_type=F32) * qdec
            kv = lax.dot_general((ki * kdec).astype(BF16), vi, (((0,), (0,)), ((), ())),
                                 preferred_element_type=F32)
            state_ref[h] = state * cdec + kv
            y_ref[rows, h * B_V_DIM:(h + 1) * B_V_DIM] = inner + cross

    gate = proj[:, 2 * D_MODEL + B_V_WIDTH:]
    gate = gate * jax.nn.sigmoid(gate)
    for h in range(B_HEADS):
        cols = slice(h * B_V_DIM, (h + 1) * B_V_DIM)
        y = y_ref[:, cols]
        mu = jnp.mean(y, axis=-1, keepdims=True)
        yc = y - mu
        var = jnp.mean(yc * yc, axis=-1, keepdims=True)
        y_ref[:, cols] = gate[:, cols] * (yc * lax.rsqrt(var + GN_EPS))
    out = jnp.dot(y_ref[...].astype(BF16), wout_ref[...], preferred_element_type=F32)
    o_ref[...] = _layer_norm(ALPHA * x + out, g_ref[...], b_ref[...], LN_EPS)


def _ret_call(x, win, cos, sin, intra, qdec, kdec, cdec, wout, g, b, batch, seq):
    blocks_per_seq = seq // TOKEN_BLOCK
    row = pl.BlockSpec((TOKEN_BLOCK, D_MODEL), lambda bi, j, *_: (bi * blocks_per_seq + j, 0))
    pos = pl.BlockSpec((TOKEN_BLOCK, B_QK_DIM // 2), lambda bi, j, *_: (j, 0))
    grid_spec = pltpu.PrefetchScalarGridSpec(
        num_scalar_prefetch=1,
        grid=(batch, blocks_per_seq),
        in_specs=[row, _resident((D_MODEL, 2 * D_MODEL + 2 * B_V_WIDTH)), pos, pos,
                  _resident((B_HEADS, CHUNK, CHUNK)), _resident((B_HEADS, CHUNK, CHUNK)),
                  _resident((B_HEADS, CHUNK, CHUNK)), _resident((B_V_WIDTH, D_MODEL)),
                  _resident((1, D_MODEL)), _resident((1, D_MODEL))],
        out_specs=row,
        scratch_shapes=[pltpu.VMEM((B_HEADS, B_QK_DIM, B_V_DIM), F32),
                        pltpu.VMEM((TOKEN_BLOCK, B_V_WIDTH), F32)],
    )
    return pl.pallas_call(
        _ret_kernel,
        grid_spec=grid_spec,
        out_shape=jax.ShapeDtypeStruct((batch * seq, D_MODEL), F32),
        compiler_params=pltpu.CompilerParams(
            dimension_semantics=("arbitrary", "arbitrary"), vmem_limit_bytes=VMEM_LIMIT_BYTES),
        name="retention_ln",
    )(cdec, x, win, cos, sin, intra, qdec, kdec, wout, g, b)


def _retention_tables(seq):
    half = B_QK_DIM // 2
    inv = ROPE_BASE ** (-jnp.arange(half, dtype=F32) / half)
    ang = jnp.arange(seq).astype(F32)[:, None] * inv[None, :]
    log_gamma = jnp.log1p(-jnp.exp2(-5.0 - jnp.arange(B_HEADS, dtype=F32)))
    idx = jnp.arange(CHUNK, dtype=F32)
    rel = idx[:, None] - idx[None, :]
    intra = jnp.where(rel >= 0, jnp.exp(log_gamma[:, None, None] * jnp.maximum(rel, 0.0)), 0.0)
    qdec = jnp.exp(log_gamma[:, None] * (idx + 1.0))
    kdec = jnp.exp(log_gamma[:, None] * (CHUNK - 1.0 - idx))
    cdec = jnp.exp(log_gamma * CHUNK)
    lanes = (B_HEADS, CHUNK, CHUNK)
    return (jnp.cos(ang), jnp.sin(ang), intra,
            jnp.broadcast_to(qdec[:, :, None], lanes), jnp.broadcast_to(kdec[:, :, None], lanes),
            cdec)


def kernel(x, ln_g, ln_b, ffn_w_gate, ffn_w_up, ffn_w_down, a_w_in, a_ln_g, a_ln_b, a_w_s, a_b_s,
           a_w_out, b_w_in, b_w_out):
    batch, seq, _ = x.shape
    h = x.reshape(batch * seq, D_MODEL)
    vec = lambda p: p.reshape(1, -1)
    tables = _retention_tables(seq)
    for i in range(DEPTH):
        ffn = lambda t, k: _ffn_call(
            t, ffn_w_gate[i, k].astype(BF16), ffn_w_up[i, k].astype(BF16),
            ffn_w_down[i, k].astype(BF16), vec(ln_g[i, 2 * k]), vec(ln_b[i, 2 * k]))
        h = ffn(h, 0)
        j = i // 2
        if i % 2 == 0:
            h = _sgu_call(h, a_w_in[j].astype(BF16), vec(a_ln_g[j]), vec(a_ln_b[j]), a_w_s[j],
                          a_b_s[j].T, a_w_out[j].astype(BF16), vec(ln_g[i, 1]), vec(ln_b[i, 1]))
        else:
            cos, sin, intra, qdec, kdec, cdec = tables
            h = _ret_call(h, b_w_in[j].astype(BF16), cos, sin, intra, qdec, kdec, cdec,
                          b_w_out[j].astype(BF16), vec(ln_g[i, 1]), vec(ln_b[i, 1]), batch, seq)
        h = ffn(h, 1)
    return h.reshape(batch, seq, D_MODEL)
```
